```python
import math
import jax, jax.numpy as jnp
from jax import lax
import numpy as np

D_MODEL = 1024
BATCH = 8
SEQ = 2048
DEPTH = 2
DEC_BATCH = 128
DEC_SEQ = 8
PAST_LEN = 8192
PAGE_SIZE = 128

N_A_LAYERS = DEPTH // 2
N_B_LAYERS = DEPTH - N_A_LAYERS

SSM_EXPAND = 2
D_INNER = SSM_EXPAND * D_MODEL
SSM_HEAD_DIM = 64
N_SSM_HEADS = D_INNER // SSM_HEAD_DIM
D_STATE = 128
N_SSM_GROUPS = 8
CONV_WIDTH = 4
CONV_DIM = D_INNER + 2 * N_SSM_GROUPS * D_STATE
IN_PROJ_DIM = D_INNER + CONV_DIM + N_SSM_HEADS
SSD_CHUNK = 128

HEAD_DIM = 64
N_HEADS = D_MODEL // HEAD_DIM
N_KV_HEADS = 4
Q_PER_KV = N_HEADS // N_KV_HEADS
WINDOW = 128
REL_BUCKETS = 32
REL_MAX_DIST = 128

N_EXPERTS = 32
TOP_K = 4
D_EXPERT = D_MODEL
SWIGLU_LIMIT = 7.0
SWIGLU_ALPHA = 1.702

NORM_EPS = 1e-6
NEG_BIG = -1e30

kernel_name = "yoco_mamba2_swa_sink_moe_step"

F32 = jnp.float32


def rmsnorm(x, g):
    xf = x.astype(F32)
    y = xf * lax.rsqrt(jnp.mean(xf * xf, axis=-1, keepdims=True) + NORM_EPS)
    return (y * g.astype(F32)).astype(x.dtype)


def causal_conv_silu(xbc, buf, w, b):
    L = xbc.shape[1]
    xp = jnp.concatenate([buf.astype(xbc.dtype), xbc], axis=1)
    out = b
    for k in range(CONV_WIDTH):
        out = out + xp[:, k:k + L] * w[k]
    return jax.nn.silu(out), xp[:, L:]


def ssd_scan(x, dt, A, Bm, Cm, D_skip, h0):
    b, L, H, P = x.shape
    G, N = Bm.shape[2], Bm.shape[3]
    R = H // G
    Q = SSD_CHUNK if L % SSD_CHUNK == 0 else L
    nc = L // Q
    xc = x.astype(F32).reshape(b, nc, Q, G, R, P)
    dtc = dt.astype(F32).reshape(b, nc, Q, G, R)
    Bc = Bm.astype(F32).reshape(b, nc, Q, G, N)
    Cc = Cm.astype(F32).reshape(b, nc, Q, G, N)
    a_cum = jnp.cumsum(dtc * A.reshape(G, R), axis=2)
    seg = a_cum[:, :, :, None] - a_cum[:, :, None, :]
    causal = jnp.tril(jnp.ones((Q, Q), dtype=bool))[:, :, None, None]
    decay = jnp.exp(jnp.where(causal, seg, -jnp.inf))
    cb = jnp.einsum('bctgn,bcsgn->bctsg', Cc, Bc)
    m = cb[..., None] * decay * dtc[:, :, None]
    y_diag = jnp.einsum('bctsgr,bcsgrp->bctgrp', m, xc)
    decay_end = jnp.exp(a_cum[:, :, -1:] - a_cum)
    xw = xc * (decay_end * dtc)[..., None]
    states = jnp.einsum('bcsgn,bcsgrp->bcgrpn', Bc, xw)
    chunk_decay = jnp.exp(a_cum[:, :, -1])

    def step(h, inp):
        s, d = inp
        return d[..., None, None] * h + s, h

    h_init = h0.astype(F32).reshape(b, G, R, P, N)
    h_final, h_prev = lax.scan(step, h_init, (jnp.swapaxes(states, 0, 1), jnp.swapaxes(chunk_decay, 0, 1)))
    h_prev = jnp.swapaxes(h_prev, 0, 1)
    y_off = jnp.einsum('bctgn,bcgrpn->bctgrp', Cc, h_prev) * jnp.exp(a_cum)[..., None]
    y = y_diag + y_off + D_skip.astype(F32).reshape(G, R)[..., None] * xc
    return y.reshape(b, L, H, P), h_final.reshape(b, H, P, N)


def mamba_mixer(xn, conv_buf, h0, w_in, conv_w, conv_b, dt_bias, A_log, D_skip, out_norm_g, w_out):
    b, L, _ = xn.shape
    GN = N_SSM_GROUPS * D_STATE
    zxbcdt = xn @ w_in
    z = zxbcdt[..., :D_INNER]
    xbc = zxbcdt[..., D_INNER:D_INNER + CONV_DIM]
    dt = zxbcdt[..., D_INNER + CONV_DIM:]
    xbc_act, new_buf = causal_conv_silu(xbc, conv_buf, conv_w, conv_b)
    xs = xbc_act[..., :D_INNER].reshape(b, L, N_SSM_HEADS, SSM_HEAD_DIM)
    Bm = xbc_act[..., D_INNER:D_INNER + GN].reshape(b, L, N_SSM_GROUPS, D_STATE)
    Cm = xbc_act[..., D_INNER + GN:].reshape(b, L, N_SSM_GROUPS, D_STATE)
    dt = jax.nn.softplus(dt.astype(F32) + dt_bias.astype(F32))
    A = -jnp.exp(A_log.astype(F32))
    y, h_final = ssd_scan(xs, dt, A, Bm, Cm, D_skip, h0)
    y = y.reshape(b, L, D_INNER) * jax.nn.silu(z.astype(F32))
    yg = y.reshape(b, L, N_SSM_GROUPS, D_INNER // N_SSM_GROUPS)
    yg = yg * lax.rsqrt(jnp.mean(yg * yg, axis=-1, keepdims=True) + NORM_EPS)
    y = (yg.reshape(b, L, D_INNER) * out_norm_g.astype(F32)).astype(xn.dtype)
    return y @ w_out, new_buf, h_final


def rel_position_bias(dist, table):
    n = jnp.maximum(dist, 0)
    max_exact = REL_BUCKETS // 2
    nf = jnp.maximum(n, 1).astype(F32)
    large = max_exact + (jnp.log(nf / max_exact) / math.log(REL_MAX_DIST / max_exact)
                         * (REL_BUCKETS - max_exact)).astype(jnp.int32)
    large = jnp.minimum(large, REL_BUCKETS - 1)
    bucket = jnp.where(n < max_exact, n, large)
    bias = table.astype(F32)[bucket]
    return jnp.transpose(bias, (2, 0, 1)).reshape(N_KV_HEADS, Q_PER_KV, *dist.shape)


def attn_core(q, k, v, bias, mask, sinks):
    s = jnp.einsum('bnqgrd,bnkgd->bngrqk', q.astype(F32), k.astype(F32)) * (HEAD_DIM ** -0.5) + bias
    s = jnp.where(mask[None, :, None, None], s, NEG_BIG)
    sink = jnp.broadcast_to(sinks.astype(F32).reshape(N_KV_HEADS, Q_PER_KV)[None, None, :, :, None, None],
                            s.shape[:-1] + (1,))
    p = jax.nn.softmax(jnp.concatenate([s, sink], axis=-1), axis=-1)[..., :-1]
    out = jnp.einsum('bngrqk,bnkgd->bnqgrd', p, v.astype(F32))
    return out.astype(q.dtype)


def window_attention(xn, k, v, buf_k, buf_v, w_q, q_norm_g, sinks, w_o, rel_table):
    b, L, _ = xn.shape
    q = rmsnorm((xn @ w_q).reshape(b, L, N_HEADS, HEAD_DIM), q_norm_g)
    q = q.reshape(b, L, N_KV_HEADS, Q_PER_KV, HEAD_DIM)
    if buf_k is None:
        W = WINDOW
        nb = L // W
        qb = q.reshape(b, nb, W, N_KV_HEADS, Q_PER_KV, HEAD_DIM)

        def band(t):
            tp = jnp.concatenate([jnp.zeros_like(t[:, :W]), t], axis=1).reshape(b, nb + 1, W, N_KV_HEADS, HEAD_DIM)
            return jnp.concatenate([tp[:, :-1], tp[:, 1:]], axis=2)

        i = jnp.arange(W)[:, None]
        j = jnp.arange(2 * W)[None, :]
        dist = W + i - j
        kpos = (jnp.arange(nb)[:, None, None] - 1) * W + j[None]
        mask = (dist >= 0)[None] & (dist < WINDOW)[None] & (kpos >= 0)
        out = attn_core(qb, band(k), band(v), rel_position_bias(dist, rel_table), mask, sinks)
    else:
        Wb = buf_k.shape[1]
        kk = jnp.concatenate([buf_k.astype(k.dtype), k], axis=1)[:, None]
        vv = jnp.concatenate([buf_v.astype(v.dtype), v], axis=1)[:, None]
        i = jnp.arange(L)[:, None]
        j = jnp.arange(Wb + L)[None, :]
        dist = Wb + i - j
        mask = ((dist >= 0) & (dist < WINDOW))[None]
        out = attn_core(q[:, None], kk, vv, rel_position_bias(dist, rel_table), mask, sinks)
    return out.reshape(b, L, N_HEADS * HEAD_DIM) @ w_o


def shared_kv(h, kv_norm_g, w_kv, k_norm_g):
    b, L, _ = h.shape
    kv = rmsnorm(h, kv_norm_g) @ w_kv
    k = rmsnorm(kv[..., :N_KV_HEADS * HEAD_DIM].reshape(b, L, N_KV_HEADS, HEAD_DIM), k_norm_g)
    v = kv[..., N_KV_HEADS * HEAD_DIM:].reshape(b, L, N_KV_HEADS, HEAD_DIM)
    return k, v


def moe(xn, w_router, b_router, w_gate, b_gate, w_up, b_up, w_down, b_down):
    shape = xn.shape
    t = xn.reshape(-1, shape[-1])
    logits = (t @ w_router + b_router).astype(F32)
    top_v, top_i = lax.top_k(logits, TOP_K)
    top_w = jax.nn.softmax(top_v, axis=-1)
    gates = jnp.sum(jax.nn.one_hot(top_i, N_EXPERTS, dtype=F32) * top_w[..., None], axis=1)
    out = jnp.zeros(t.shape, F32)
    for e in range(N_EXPERTS):
        g = jnp.minimum(t @ w_gate[e] + b_gate[e], SWIGLU_LIMIT)
        u = jnp.clip(t @ w_up[e] + b_up[e], -SWIGLU_LIMIT, SWIGLU_LIMIT)
        hmid = g * jax.nn.sigmoid(SWIGLU_ALPHA * g) * (u + 1.0)
        out = out + gates[:, e:e + 1] * (hmid @ w_down[e] + b_down[e]).astype(F32)
    return out.astype(xn.dtype).reshape(shape)


def trunk(x, conv_state, ssm_state, buf_k, buf_v, p):
    h = x
    new_conv, new_ssm = [], []
    k = v = None
    for layer in range(DEPTH):
        if layer < N_A_LAYERS:
            a = layer
            out, cbuf, hs = mamba_mixer(rmsnorm(h, p['mamba_norm_g'][a]), conv_state[a], ssm_state[a],
                                        p['mamba_w_in'][a], p['mamba_conv_w'][a], p['mamba_conv_b'][a],
                                        p['mamba_dt_bias'][a], p['mamba_A_log'][a], p['mamba_D'][a],
                                        p['mamba_out_norm_g'][a], p['mamba_w_out'][a])
            h = h + out
            new_conv.append(cbuf)
            new_ssm.append(hs)
        else:
            if layer == N_A_LAYERS:
                k, v = shared_kv(h, p['kv_norm_g'], p['w_kv'], p['k_norm_g'])
            bl = layer - N_A_LAYERS
            h = h + window_attention(rmsnorm(h, p['attn_norm_g'][bl]), k, v, buf_k, buf_v,
                                     p['w_q'][bl], p['q_norm_g'][bl], p['attn_sinks'][bl], p['w_o'][bl],
                                     p['rel_bias'])
        h = h + moe(rmsnorm(h, p['moe_norm_g'][layer]), p['moe_w_router'][layer], p['moe_b_router'][layer],
                    p['moe_w_gate'][layer], p['moe_b_gate'][layer], p['moe_w_up'][layer], p['moe_b_up'][layer],
                    p['moe_w_down'][layer], p['moe_b_down'][layer])
    if buf_k is None:
        new_k, new_v = k[:, -WINDOW:], v[:, -WINDOW:]
    else:
        Wb = buf_k.shape[1]
        new_k = jnp.concatenate([buf_k.astype(k.dtype), k], axis=1)[:, -Wb:]
        new_v = jnp.concatenate([buf_v.astype(v.dtype), v], axis=1)[:, -Wb:]
    return h, jnp.stack(new_conv), jnp.stack(new_ssm), new_k, new_v


def setup_inputs(seed: int = 0) -> dict:
    key = jax.random.key(seed)
    ks = iter(jax.random.split(key, 48))
    nrm = lambda shape, scale: jax.random.normal(next(ks), shape, F32) * scale
    gain = lambda shape: 1.0 + 0.02 * jax.random.normal(next(ks), shape, F32)
    wb = min(WINDOW, PAST_LEN)
    dt0 = jnp.exp(jax.random.uniform(next(ks), (N_A_LAYERS, N_SSM_HEADS), F32, math.log(1e-3), math.log(1e-1)))
    return {
        'x_prompt': nrm((BATCH, SEQ, D_MODEL), 1.0),
        'x_sample': nrm((DEC_BATCH, DEC_SEQ, D_MODEL), 1.0),
        'state_ssm': nrm((N_A_LAYERS, DEC_BATCH, N_SSM_HEADS, SSM_HEAD_DIM, D_STATE), 0.1),
        'state_conv': nrm((N_A_LAYERS, DEC_BATCH, CONV_WIDTH - 1, CONV_DIM), 1.0),
        'cache_k_window': nrm((DEC_BATCH, wb, N_KV_HEADS, HEAD_DIM), 1.0),
        'cache_v_window': nrm((DEC_BATCH, wb, N_KV_HEADS, HEAD_DIM), 1.0),
        'mamba_norm_g': gain((N_A_LAYERS, D_MODEL)),
        'mamba_w_in': nrm((N_A_LAYERS, D_MODEL, IN_PROJ_DIM), D_MODEL ** -0.5),
        'mamba_conv_w': nrm((N_A_LAYERS, CONV_WIDTH, CONV_DIM), CONV_WIDTH ** -0.5),
        'mamba_conv_b': nrm((N_A_LAYERS, CONV_DIM), 0.01),
        'mamba_dt_bias': dt0 + jnp.log(-jnp.expm1(-dt0)),
        'mamba_A_log': jnp.log(jax.random.uniform(next(ks), (N_A_LAYERS, N_SSM_HEADS), F32, 1.0, 16.0)),
        'mamba_D': gain((N_A_LAYERS, N_SSM_HEADS)),
        'mamba_out_norm_g': gain((N_A_LAYERS, D_INNER)),
        'mamba_w_out': nrm((N_A_LAYERS, D_INNER, D_MODEL), D_INNER ** -0.5),
        'kv_norm_g': gain((D_MODEL,)),
        'w_kv': nrm((D_MODEL, 2 * N_KV_HEADS * HEAD_DIM), D_MODEL ** -0.5),
        'k_norm_g': gain((HEAD_DIM,)),
        'attn_norm_g': gain((N_B_LAYERS, D_MODEL)),
        'w_q': nrm((N_B_LAYERS, D_MODEL, N_HEADS * HEAD_DIM), D_MODEL ** -0.5),
        'q_norm_g': gain((N_B_LAYERS, HEAD_DIM)),
        'attn_sinks': nrm((N_B_LAYERS, N_HEADS), 0.5),
        'w_o': nrm((N_B_LAYERS, N_HEADS * HEAD_DIM, D_MODEL), (N_HEADS * HEAD_DIM) ** -0.5),
        'rel_bias': nrm((REL_BUCKETS, N_HEADS), 0.5),
        'moe_norm_g': gain((DEPTH, D_MODEL)),
        'moe_w_router': nrm((DEPTH, D_MODEL, N_EXPERTS), D_MODEL ** -0.5),
        'moe_b_router': nrm((DEPTH, N_EXPERTS), 0.01),
        'moe_w_gate': nrm((DEPTH, N_EXPERTS, D_MODEL, D_EXPERT), D_MODEL ** -0.5),
        'moe_b_gate': nrm((DEPTH, N_EXPERTS, D_EXPERT), 0.01),
        'moe_w_up': nrm((DEPTH, N_EXPERTS, D_MODEL, D_EXPERT), D_MODEL ** -0.5),
        'moe_b_up': nrm((DEPTH, N_EXPERTS, D_EXPERT), 0.01),
        'moe_w_down': nrm((DEPTH, N_EXPERTS, D_EXPERT, D_MODEL), D_EXPERT ** -0.5),
        'moe_b_down': nrm((DEPTH, N_EXPERTS, D_MODEL), 0.01),
    }


def reference(x_prompt, x_sample, state_ssm, state_conv, cache_k_window, cache_v_window,
              mamba_norm_g, mamba_w_in, mamba_conv_w, mamba_conv_b, mamba_dt_bias, mamba_A_log, mamba_D,
              mamba_out_norm_g, mamba_w_out, kv_norm_g, w_kv, k_norm_g, attn_norm_g, w_q, q_norm_g,
              attn_sinks, w_o, rel_bias, moe_norm_g, moe_w_router, moe_b_router, moe_w_gate, moe_b_gate,
              moe_w_up, moe_b_up, moe_w_down, moe_b_down):
    p = dict(mamba_norm_g=mamba_norm_g, mamba_w_in=mamba_w_in, mamba_conv_w=mamba_conv_w,
             mamba_conv_b=mamba_conv_b, mamba_dt_bias=mamba_dt_bias, mamba_A_log=mamba_A_log, mamba_D=mamba_D,
             mamba_out_norm_g=mamba_out_norm_g, mamba_w_out=mamba_w_out, kv_norm_g=kv_norm_g, w_kv=w_kv,
             k_norm_g=k_norm_g, attn_norm_g=attn_norm_g, w_q=w_q, q_norm_g=q_norm_g, attn_sinks=attn_sinks,
             w_o=w_o, rel_bias=rel_bias, moe_norm_g=moe_norm_g, moe_w_router=moe_w_router,
             moe_b_router=moe_b_router, moe_w_gate=moe_w_gate, moe_b_gate=moe_b_gate, moe_w_up=moe_w_up,
             moe_b_up=moe_b_up, moe_w_down=moe_w_down, moe_b_down=moe_b_down)
    b_p = x_prompt.shape[0]
    conv0 = jnp.zeros((N_A_LAYERS, b_p, CONV_WIDTH - 1, CONV_DIM), x_prompt.dtype)
    ssm0 = jnp.zeros((N_A_LAYERS, b_p, N_SSM_HEADS, SSM_HEAD_DIM, D_STATE), F32)
    y_prompt, conv_p, ssm_p, k_p, v_p = trunk(x_prompt, conv0, ssm0, None, None, p)
    y_sample, conv_s, ssm_s, k_s, v_s = trunk(x_sample, state_conv, state_ssm, cache_k_window, cache_v_window, p)
    return (y_prompt, y_sample, ssm_p, conv_p, k_p, v_p, ssm_s, conv_s, k_s, v_s)
```

```python
import functools
import math

import jax
import jax.numpy as jnp
from jax import lax
from jax.experimental import pallas as pl
from jax.experimental.pallas import tpu as pltpu

F32 = jnp.float32
BF16 = jnp.bfloat16
HIGHEST = lax.Precision.HIGHEST

D_MODEL = 1024
BATCH = 8
SEQ = 2048
DEC_BATCH = 128
DEC_SEQ = 8
T_PROMPT = BATCH * SEQ
T_SAMPLE = DEC_BATCH * DEC_SEQ
T_ALL = T_PROMPT + T_SAMPLE

D_INNER = 2048
SSM_HEAD_DIM = 64
N_SSM_HEADS = 32
D_STATE = 128
N_SSM_GROUPS = 8
HEADS_PER_GROUP = N_SSM_HEADS // N_SSM_GROUPS
CONV_WIDTH = 4
CONV_DIM = D_INNER + 2 * N_SSM_GROUPS * D_STATE
IN_PROJ_DIM = D_INNER + CONV_DIM + N_SSM_HEADS
IN_PROJ_PAD = 6272
SSD_CHUNK = 128

HEAD_DIM = 64
N_HEADS = 16
N_KV_HEADS = 4
Q_PER_KV = 4
KV_DIM = N_KV_HEADS * HEAD_DIM
WINDOW = 128
REL_BUCKETS = 32
REL_MAX_DIST = 128

N_EXPERTS = 32
TOP_K = 4
SWIGLU_LIMIT = 7.0
SWIGLU_ALPHA = 1.702

NORM_EPS = 1e-6
NEG_BIG = -1e30

LANES = 128
TOKEN_TILE = 512
EXPERT_TILE = 512
ROW_DMA_TILE = 256
VMEM_LIMIT = 56 * 1024 * 1024


def _cparams(sem):
    return pltpu.CompilerParams(dimension_semantics=sem, vmem_limit_bytes=VMEM_LIMIT)


def _rms(x, g):
    return x * lax.rsqrt(jnp.mean(x * x, axis=-1, keepdims=True) + NORM_EPS) * g


def _sigmoid(x):
    return 1.0 / (1.0 + jnp.exp(-x))


def _matmul_kernel(*refs, norm, residual):
    refs = list(refs)
    x_ref = refs.pop(0)
    g_ref = refs.pop(0) if norm else None
    w_ref = refs.pop(0)
    r_ref = refs.pop(0) if residual else None
    o_ref, xs_ref = refs

    @pl.when(pl.program_id(1) == 0)
    def _():
        x = x_ref[...].astype(F32)
        if norm:
            x = _rms(x, g_ref[...])
        xs_ref[...] = x.astype(BF16)

    acc = jnp.dot(xs_ref[...], w_ref[...], preferred_element_type=F32)
    if residual:
        acc = acc + r_ref[...]
    o_ref[...] = acc.astype(o_ref.dtype)


def _matmul(x, w, *, g=None, res=None, tn, name):
    m, k = x.shape
    n = w.shape[1]
    tm = TOKEN_TILE
    in_specs = [pl.BlockSpec((tm, k), lambda i, j: (i, 0))]
    args = [x]
    if g is not None:
        in_specs.append(pl.BlockSpec((1, k), lambda i, j: (0, 0)))
        args.append(g.reshape(1, k).astype(F32))
    in_specs.append(pl.BlockSpec((k, tn), lambda i, j: (0, j)))
    args.append(w)
    if res is not None:
        in_specs.append(pl.BlockSpec((tm, tn), lambda i, j: (i, j)))
        args.append(res)
    return pl.pallas_call(
        functools.partial(_matmul_kernel, norm=g is not None, residual=res is not None),
        out_shape=jax.ShapeDtypeStruct((m, n), F32),
        grid=(m // tm, n // tn),
        in_specs=in_specs,
        out_specs=pl.BlockSpec((tm, tn), lambda i, j: (i, j)),
        scratch_shapes=[pltpu.VMEM((tm, k), BF16)],
        compiler_params=_cparams(("parallel", "arbitrary")),
        name=name,
    )(*args)


def _ssd_kernel(z_ref, x_ref, bc_ref, dt_ref, cw_ref, cb_ref, dtb_ref, alog_ref, dexp_ref, ng_ref,
                cinit_ref, hinit_ref, ybuf_ref, y_ref, cstate_ref, hstate_ref,
                xpad_ref, act_ref, h_ref, *, rows):
    del ybuf_ref
    q = SSD_CHUNK
    c = pl.program_id(1)
    pad = 8

    @pl.when(c == 0)
    def _():
        xpad_ref[pad - 3:pad, :] = cinit_ref[0]
        h_ref[...] = hinit_ref[0]

    xpad_ref[pad:pad + rows, 0:D_INNER] = x_ref[...]
    xpad_ref[pad:pad + rows, D_INNER:CONV_DIM] = bc_ref[...]
    if rows < q:
        xpad_ref[pad + rows:pad + q, :] = jnp.zeros((q - rows, CONV_DIM), F32)

    slab = 512
    for s in range(CONV_DIM // slab):
        cs = slice(s * slab, (s + 1) * slab)
        acc = cb_ref[:, cs]
        for k in range(CONV_WIDTH):
            acc = acc + xpad_ref[pad - 3 + k:pad - 3 + k + q, cs] * cw_ref[k:k + 1, cs]
        act_ref[:, cs] = acc * _sigmoid(acc)

    tail = xpad_ref[pad + rows - 3:pad + rows, :]
    cstate_ref[0] = tail
    xpad_ref[pad - 3:pad, :] = tail

    dt_raw = dt_ref[...]
    if rows < q:
        dt_raw = jnp.concatenate([dt_raw, jnp.zeros((q - rows, LANES), F32)], axis=0)
    xb = dt_raw + dtb_ref[...]
    dt = jnp.maximum(xb, 0.0) + jnp.log1p(jnp.exp(-jnp.abs(xb)))
    row_id = lax.broadcasted_iota(jnp.int32, (q, LANES), 0)
    if rows < q:
        dt = jnp.where(row_id < rows, dt, 0.0)
    a = dt * (-jnp.exp(alog_ref[...]))
    col_id = lax.broadcasted_iota(jnp.int32, (q, q), 1)
    row_q = lax.broadcasted_iota(jnp.int32, (q, q), 0)
    causal = row_q >= col_id
    a_cum = jnp.dot(causal.astype(F32), a, precision=HIGHEST, preferred_element_type=F32)
    a_cum_t = a_cum.T
    dt_t = dt.T
    w_end_t = (jnp.exp(a_cum[q - 1:q, :] - a_cum) * dt).T
    chunk_decay = jnp.broadcast_to(jnp.exp(a_cum_t[:, q - 1:q]), (LANES, D_STATE))
    lane_lo = lax.broadcasted_iota(jnp.int32, (q, LANES), 1) < SSM_HEAD_DIM

    gw = HEADS_PER_GROUP * SSM_HEAD_DIM
    for g in range(N_SSM_GROUPS):
        b_g = act_ref[:, D_INNER + g * D_STATE:D_INNER + (g + 1) * D_STATE].astype(BF16)
        c_g = act_ref[:, D_INNER + N_SSM_GROUPS * D_STATE + g * D_STATE:
                      D_INNER + N_SSM_GROUPS * D_STATE + (g + 1) * D_STATE].astype(BF16)
        x_g = act_ref[:, g * gw:(g + 1) * gw]
        h_prev = h_ref[g * gw:(g + 1) * gw, :]
        cb = lax.dot_general(c_g, b_g, (((1,), (1,)), ((), ())), preferred_element_type=F32)
        y_off = lax.dot_general(c_g, h_prev.astype(BF16), (((1,), (1,)), ((), ())),
                                preferred_element_type=F32)
        y_pairs = []
        for pr in range(HEADS_PER_GROUP // 2):
            x_pair = x_g[:, pr * LANES:(pr + 1) * LANES]
            x_pair_bf = x_pair.astype(BF16)
            halves = []
            for half in range(2):
                h = g * HEADS_PER_GROUP + pr * 2 + half
                a_col = jnp.broadcast_to(a_cum[:, h:h + 1], (q, q))
                decay = jnp.exp(jnp.where(causal, a_col - a_cum_t[h:h + 1, :], -jnp.inf))
                m = (cb * decay * dt_t[h:h + 1, :]).astype(BF16)
                y_d = jnp.dot(m, x_pair_bf, preferred_element_type=F32)
                halves.append(y_d + y_off[:, pr * LANES:(pr + 1) * LANES] * jnp.exp(a_col))
            y_pair = jnp.where(lane_lo, halves[0], halves[1])
            y_pairs.append(y_pair + dexp_ref[:, g * gw + pr * LANES:g * gw + (pr + 1) * LANES] * x_pair)
        y_g = jnp.concatenate(y_pairs, axis=1)

        x_t = x_g.T
        xw_t = jnp.concatenate(
            [x_t[r * SSM_HEAD_DIM:(r + 1) * SSM_HEAD_DIM, :] *
             w_end_t[g * HEADS_PER_GROUP + r:g * HEADS_PER_GROUP + r + 1, :] for r in range(HEADS_PER_GROUP)],
            axis=0).astype(BF16)
        cd_g = jnp.concatenate(
            [jnp.broadcast_to(chunk_decay[g * HEADS_PER_GROUP + r:g * HEADS_PER_GROUP + r + 1, :],
                              (SSM_HEAD_DIM, D_STATE)) for r in range(HEADS_PER_GROUP)], axis=0)
        h_ref[g * gw:(g + 1) * gw, :] = cd_g * h_prev + jnp.dot(xw_t, b_g, preferred_element_type=F32)

        z_g = z_ref[:, g * gw:(g + 1) * gw]
        if rows < q:
            y_g = y_g[:rows]
        y_g = y_g * (z_g * _sigmoid(z_g))
        y_g = _rms(y_g, ng_ref[:, g * gw:(g + 1) * gw])
        y_ref[:, g * gw:(g + 1) * gw] = y_g.astype(y_ref.dtype)

    hstate_ref[0] = h_ref[...]


def _ssd(zx, y_buf, conv_init, h_init, params, *, nb, nchunk, rows, row_block0, name):
    cw, cb, dtb, alog, dexp, ng = params
    rb = lambda b, c: row_block0 + b * nchunk + c
    in_specs = [
        pl.BlockSpec((rows, D_INNER), lambda b, c: (rb(b, c), 0)),
        pl.BlockSpec((rows, D_INNER), lambda b, c: (rb(b, c), 1)),
        pl.BlockSpec((rows, D_INNER), lambda b, c: (rb(b, c), 2)),
        pl.BlockSpec((rows, LANES), lambda b, c: (rb(b, c), (D_INNER + CONV_DIM) // LANES)),
        pl.BlockSpec((CONV_WIDTH, CONV_DIM), lambda b, c: (0, 0)),
        pl.BlockSpec((1, CONV_DIM), lambda b, c: (0, 0)),
        pl.BlockSpec((1, LANES), lambda b, c: (0, 0)),
        pl.BlockSpec((1, LANES), lambda b, c: (0, 0)),
        pl.BlockSpec((1, D_INNER), lambda b, c: (0, 0)),
        pl.BlockSpec((1, D_INNER), lambda b, c: (0, 0)),
        pl.BlockSpec((1, CONV_WIDTH - 1, CONV_DIM), lambda b, c: (b, 0, 0)),
        pl.BlockSpec((1, D_INNER, D_STATE), lambda b, c: (b, 0, 0)),
        pl.BlockSpec(memory_space=pl.ANY),
    ]
    out_shape = (
        jax.ShapeDtypeStruct(y_buf.shape, y_buf.dtype),
        jax.ShapeDtypeStruct((nb, CONV_WIDTH - 1, CONV_DIM), F32),
        jax.ShapeDtypeStruct((nb, D_INNER, D_STATE), F32),
    )
    out_specs = (
        pl.BlockSpec((rows, D_INNER), lambda b, c: (rb(b, c), 0)),
        pl.BlockSpec((1, CONV_WIDTH - 1, CONV_DIM), lambda b, c: (b, 0, 0)),
        pl.BlockSpec((1, D_INNER, D_STATE), lambda b, c: (b, 0, 0)),
    )
    return pl.pallas_call(
        functools.partial(_ssd_kernel, rows=rows),
        out_shape=out_shape,
        grid=(nb, nchunk),
        in_specs=in_specs,
        out_specs=out_specs,
        scratch_shapes=[pltpu.VMEM((SSD_CHUNK + 8, CONV_DIM), F32),
                        pltpu.VMEM((SSD_CHUNK, CONV_DIM), F32),
                        pltpu.VMEM((D_INNER, D_STATE), F32)],
        input_output_aliases={12: 0},
        compiler_params=_cparams(("parallel", "arbitrary")),
        name=name,
    )(zx, zx, zx, zx, cw, cb, dtb, alog, dexp, ng, conv_init, h_init, y_buf)


def _kv_kernel(x_ref, g_ref, w_ref, kg_ref, bd_ref, k_ref, v_ref):
    xn = _rms(x_ref[...], g_ref[...]).astype(BF16)
    kv = jnp.dot(xn, w_ref[...], preferred_element_type=F32)
    k = kv[:, :KV_DIM]
    ms = jnp.dot(k * k, bd_ref[...], precision=HIGHEST, preferred_element_type=F32)
    k_ref[...] = k * lax.rsqrt(ms + NORM_EPS) * kg_ref[...]
    v_ref[...] = kv[:, KV_DIM:]


def _shared_kv(h, g, w_kv, k_norm_g):
    m = h.shape[0]
    tm = TOKEN_TILE
    head_of = jnp.arange(KV_DIM) // HEAD_DIM
    block_mean = (head_of[:, None] == head_of[None, :]).astype(F32) / HEAD_DIM
    return pl.pallas_call(
        _kv_kernel,
        out_shape=(jax.ShapeDtypeStruct((m, KV_DIM), F32), jax.ShapeDtypeStruct((m, KV_DIM), F32)),
        grid=(m // tm,),
        in_specs=[pl.BlockSpec((tm, D_MODEL), lambda i: (i, 0)),
                  pl.BlockSpec((1, D_MODEL), lambda i: (0, 0)),
                  pl.BlockSpec((D_MODEL, 2 * KV_DIM), lambda i: (0, 0)),
                  pl.BlockSpec((1, KV_DIM), lambda i: (0, 0)),
                  pl.BlockSpec((KV_DIM, KV_DIM), lambda i: (0, 0))],
        out_specs=(pl.BlockSpec((tm, KV_DIM), lambda i: (i, 0)), pl.BlockSpec((tm, KV_DIM), lambda i: (i, 0))),
        compiler_params=_cparams(("parallel",)),
        name="shared_kv",
    )(h, g.reshape(1, D_MODEL), w_kv.astype(BF16), jnp.tile(k_norm_g, N_KV_HEADS).reshape(1, KV_DIM), block_mean)


def _attn_kernel(q_ref, kp_ref, kc_ref, vp_ref, vc_ref, bias_ref, sink_ref, qg_ref, obuf_ref, o_ref,
                 kband_ref, vband_ref, *, lq, lc):
    del obuf_ref
    kband_ref[0:WINDOW, :] = kp_ref[...]
    vband_ref[0:WINDOW, :] = vp_ref[...]
    kband_ref[WINDOW:WINDOW + lc, :] = kc_ref[...]
    vband_ref[WINDOW:WINDOW + lc, :] = vc_ref[...]
    if lc < WINDOW:
        kband_ref[WINDOW + lc:, :] = jnp.zeros((WINDOW - lc, KV_DIM), F32)
        vband_ref[WINDOW + lc:, :] = jnp.zeros((WINDOW - lc, KV_DIM), F32)

    outs = []
    for g in range(N_KV_HEADS):
        k_g = kband_ref[:, g * HEAD_DIM:(g + 1) * HEAD_DIM].astype(BF16)
        v_g = vband_ref[:, g * HEAD_DIM:(g + 1) * HEAD_DIM].astype(BF16)
        qs, sinks = [], []
        for r in range(Q_PER_KV):
            h = g * Q_PER_KV + r
            qs.append(_rms(q_ref[:, h * HEAD_DIM:(h + 1) * HEAD_DIM], qg_ref[...]))
            sinks.append(jnp.broadcast_to(sink_ref[h:h + 1, 0:1], (lq, 1)))
        q_g = jnp.concatenate(qs, axis=0).astype(BF16)
        sink = jnp.concatenate(sinks, axis=0)
        s = lax.dot_general(q_g, k_g, (((1,), (1,)), ((), ())), preferred_element_type=F32)
        s = s * (HEAD_DIM ** -0.5) + bias_ref[0, g * Q_PER_KV:(g + 1) * Q_PER_KV].reshape(Q_PER_KV * lq, 2 * WINDOW)
        m = jnp.maximum(jnp.max(s, axis=-1, keepdims=True), sink)
        p = jnp.exp(s - m)
        den = jnp.sum(p, axis=-1, keepdims=True) + jnp.exp(sink - m)
        o = jnp.dot(p.astype(BF16), v_g, preferred_element_type=F32) / den
        for r in range(Q_PER_KV):
            outs.append(o[r * lq:(r + 1) * lq])
    o_ref[...] = jnp.concatenate(outs, axis=1).astype(o_ref.dtype)


def _attention(q_all, k_prev, k_cur, v_prev, v_cur, bias, sinks, q_norm_g, o_buf, *,
               grid, lq, lc, q_map, prev_map, cur_map, bias_map, name):
    in_specs = [
        pl.BlockSpec((lq, D_MODEL), q_map),
        pl.BlockSpec((WINDOW, KV_DIM), prev_map),
        pl.BlockSpec((lc, KV_DIM), cur_map),
        pl.BlockSpec((WINDOW, KV_DIM), prev_map),
        pl.BlockSpec((lc, KV_DIM), cur_map),
        pl.BlockSpec((1, N_HEADS, lq, 2 * WINDOW), bias_map),
        pl.BlockSpec((N_HEADS, LANES), lambda *_: (0, 0)),
        pl.BlockSpec((1, HEAD_DIM), lambda *_: (0, 0)),
        pl.BlockSpec(memory_space=pl.ANY),
    ]
    return pl.pallas_call(
        functools.partial(_attn_kernel, lq=lq, lc=lc),
        out_shape=jax.ShapeDtypeStruct(o_buf.shape, o_buf.dtype),
        grid=grid,
        in_specs=in_specs,
        out_specs=pl.BlockSpec((lq, D_MODEL), q_map),
        scratch_shapes=[pltpu.VMEM((2 * WINDOW, KV_DIM), F32), pltpu.VMEM((2 * WINDOW, KV_DIM), F32)],
        input_output_aliases={8: 0},
        compiler_params=_cparams(("parallel",) * len(grid)),
        name=name,
    )(q_all, k_prev, k_cur, v_prev, v_cur, bias, sinks, q_norm_g.reshape(1, HEAD_DIM), o_buf)


def _rel_bias(dist, table):
    n = jnp.maximum(dist, 0)
    max_exact = REL_BUCKETS // 2
    nf = jnp.maximum(n, 1).astype(F32)
    large = max_exact + (jnp.log(nf / max_exact) / math.log(REL_MAX_DIST / max_exact)
                         * (REL_BUCKETS - max_exact)).astype(jnp.int32)
    large = jnp.minimum(large, REL_BUCKETS - 1)
    bucket = jnp.where(n < max_exact, n, large)
    return jnp.transpose(table.astype(F32)[bucket], (2, 0, 1))


def _router_kernel(h_ref, g_ref, wr_ref, br_ref, xn_ref, idx_ref, gate_ref):
    xn = _rms(h_ref[...], g_ref[...])
    xn_ref[...] = xn
    logits = jnp.dot(xn, wr_ref[...], precision=HIGHEST, preferred_element_type=F32) + br_ref[...]
    lane = lax.broadcasted_iota(jnp.int32, logits.shape, 1)
    vals, idxs = [], []
    for _ in range(TOP_K):
        m = jnp.max(logits, axis=-1, keepdims=True)
        idx = jnp.min(jnp.where(logits == m, lane, LANES), axis=-1, keepdims=True)
        logits = jnp.where(lane == idx, -jnp.inf, logits)
        vals.append(m)
        idxs.append(idx)
    es = [jnp.exp(v - vals[0]) for v in vals]
    den = es[0] + es[1] + es[2] + es[3]
    idx_out = jnp.zeros(logits.shape, jnp.int32)
    gate_out = jnp.zeros(logits.shape, F32)
    for k in range(TOP_K):
        idx_out = jnp.where(lane == k, idxs[k], idx_out)
        gate_out = jnp.where(lane == k, es[k] / den, gate_out)
    idx_ref[...] = idx_out
    gate_ref[...] = gate_out


def _router(h, g, w_router, b_router):
    m = h.shape[0]
    tm = TOKEN_TILE
    wr = jnp.pad(w_router, ((0, 0), (0, LANES - N_EXPERTS)))
    br = jnp.concatenate([b_router, jnp.full((LANES - N_EXPERTS,), NEG_BIG, F32)]).reshape(1, LANES)
    row = lambda i: (i, 0)
    fix = lambda i: (0, 0)
    return pl.pallas_call(
        _router_kernel,
        out_shape=(jax.ShapeDtypeStruct((m, D_MODEL), F32),
                   jax.ShapeDtypeStruct((m, LANES), jnp.int32),
                   jax.ShapeDtypeStruct((m, LANES), F32)),
        grid=(m // tm,),
        in_specs=[pl.BlockSpec((tm, D_MODEL), row), pl.BlockSpec((1, D_MODEL), fix),
                  pl.BlockSpec((D_MODEL, LANES), fix), pl.BlockSpec((1, LANES), fix)],
        out_specs=(pl.BlockSpec((tm, D_MODEL), row), pl.BlockSpec((tm, LANES), row), pl.BlockSpec((tm, LANES), row)),
        compiler_params=_cparams(("parallel",)),
        name="moe_router",
    )(h, g.reshape(1, D_MODEL), wr, br)


def _row_copy(src_hbm, src_row, dst, dst_row, sem):
    return pltpu.make_async_copy(src_hbm.at[pl.ds(src_row, 1)], dst.at[pl.ds(dst_row, 1)], sem)


def _dispatch_kernel(pos_ref, x_hbm, init_hbm, xs_hbm, sem):
    del init_hbm
    base = pl.program_id(0) * ROW_DMA_TILE

    def issue(j, carry):
        for k in range(TOP_K):
            _row_copy(x_hbm, base + j, xs_hbm, pos_ref[j * TOP_K + k], sem).start()
        return carry

    def drain(j, carry):
        for k in range(TOP_K):
            _row_copy(x_hbm, base + j, xs_hbm, pos_ref[j * TOP_K + k], sem).wait()
        return carry

    lax.fori_loop(0, ROW_DMA_TILE, issue, 0)
    lax.fori_loop(0, ROW_DMA_TILE, drain, 0)


def _dispatch(xn, pos_flat, n_rows):
    m = xn.shape[0]
    return pl.pallas_call(
        _dispatch_kernel,
        out_shape=jax.ShapeDtypeStruct((n_rows, D_MODEL), xn.dtype),
        grid=(m // ROW_DMA_TILE,),
        in_specs=[pl.BlockSpec((ROW_DMA_TILE * TOP_K,), lambda i: (i,), memory_space=pltpu.SMEM),
                  pl.BlockSpec(memory_space=pl.ANY),
                  pl.BlockSpec(memory_space=pl.ANY)],
        out_specs=pl.BlockSpec(memory_space=pl.ANY),
        scratch_shapes=[pltpu.SemaphoreType.DMA],
        input_output_aliases={2: 0},
        compiler_params=_cparams(("arbitrary",)),
        name="moe_dispatch",
    )(pos_flat, xn, jnp.zeros((n_rows, D_MODEL), xn.dtype))


def _expert_kernel(te_ref, nt_ref, x_ref, wg_ref, bg_ref, wu_ref, bu_ref, wd_ref, bd_ref, y_ref):
    del te_ref
    i = pl.program_id(0)

    @pl.when(i < nt_ref[0])
    def _():
        x = x_ref[...].astype(BF16)
        gt = jnp.minimum(jnp.dot(x, wg_ref[0], preferred_element_type=F32) + bg_ref[0], SWIGLU_LIMIT)
        up = jnp.clip(jnp.dot(x, wu_ref[0], preferred_element_type=F32) + bu_ref[0], -SWIGLU_LIMIT, SWIGLU_LIMIT)
        mid = gt * _sigmoid(SWIGLU_ALPHA * gt) * (up + 1.0)
        y_ref[...] = jnp.dot(mid.astype(BF16), wd_ref[0], preferred_element_type=F32) + bd_ref[0]

    @pl.when(i >= nt_ref[0])
    def _():
        y_ref[...] = jnp.zeros(y_ref.shape, y_ref.dtype)


def _experts(xs, tile_expert, n_tiles_used, wg, bg, wu, bu, wd, bd):
    n_rows = xs.shape[0]
    tm = EXPERT_TILE
    wspec = pl.BlockSpec((1, D_MODEL, D_MODEL), lambda i, te, nt: (te[i], 0, 0))
    bspec = pl.BlockSpec((1, 1, D_MODEL), lambda i, te, nt: (te[i], 0, 0))
    xspec = pl.BlockSpec((tm, D_MODEL), lambda i, te, nt: (i, 0))
    return pl.pallas_call(
        _expert_kernel,
        out_shape=jax.ShapeDtypeStruct((n_rows, D_MODEL), F32),
        grid_spec=pltpu.PrefetchScalarGridSpec(
            num_scalar_prefetch=2,
            grid=(n_rows // tm,),
            in_specs=[xspec, wspec, bspec, wspec, bspec, wspec, bspec],
            out_specs=xspec),
        compiler_params=_cparams(("arbitrary",)),
        name="moe_experts",
    )(tile_expert, n_tiles_used, xs, wg, bg, wu, bu, wd, bd)


def _combine_kernel(pos_ref, ys_hbm, gate_ref, res_ref, o_ref, buf_ref, sem):
    def issue(j, carry):
        for k in range(TOP_K):
            _row_copy(ys_hbm, pos_ref[j * TOP_K + k], buf_ref.at[k], j, sem).start()
        return carry

    def drain(j, carry):
        for k in range(TOP_K):
            _row_copy(ys_hbm, pos_ref[j * TOP_K + k], buf_ref.at[k], j, sem).wait()
        return carry

    lax.fori_loop(0, ROW_DMA_TILE, issue, 0)
    lax.fori_loop(0, ROW_DMA_TILE, drain, 0)
    acc = res_ref[...]
    for k in range(TOP_K):
        acc = acc + gate_ref[:, k:k + 1] * buf_ref[k]
    o_ref[...] = acc


def _combine(ys, pos_flat, gates, res):
    m = res.shape[0]
    tm = ROW_DMA_TILE
    return pl.pallas_call(
        _combine_kernel,
        out_shape=jax.ShapeDtypeStruct((m, D_MODEL), F32),
        grid=(m // tm,),
        in_specs=[pl.BlockSpec((tm * TOP_K,), lambda i: (i,), memory_space=pltpu.SMEM),
                  pl.BlockSpec(memory_space=pl.ANY),
                  pl.BlockSpec((tm, LANES), lambda i: (i, 0)),
                  pl.BlockSpec((tm, D_MODEL), lambda i: (i, 0))],
        out_specs=pl.BlockSpec((tm, D_MODEL), lambda i: (i, 0)),
        scratch_shapes=[pltpu.VMEM((TOP_K, tm, D_MODEL), F32), pltpu.SemaphoreType.DMA],
        compiler_params=_cparams(("arbitrary",)),
        name="moe_combine",
    )(pos_flat, ys, gates, res)


def _moe(h, g, w_router, b_router, wg, bg, wu, bu, wd, bd):
    m = h.shape[0]
    tm = EXPERT_TILE
    n_tiles = m * TOP_K // tm + N_EXPERTS
    xn, idx, gates = _router(h, g, w_router, b_router)

    top_i = idx[:, :TOP_K]
    onehot = (top_i[:, :, None] == jnp.arange(N_EXPERTS, dtype=jnp.int32)).astype(jnp.int32)
    sel = jnp.sum(onehot, axis=1)
    csum = jnp.cumsum(sel, axis=0)
    counts = csum[-1]
    padded = (counts + tm - 1) // tm * tm
    group_end = jnp.cumsum(padded)
    slot = (group_end - padded)[None, :] + csum - sel
    pos = jnp.sum(onehot * slot[:, None, :], axis=-1).astype(jnp.int32).reshape(-1)
    tile_start = jnp.arange(n_tiles, dtype=jnp.int32) * tm
    tile_expert = jnp.minimum(jnp.searchsorted(group_end, tile_start, side="right"), N_EXPERTS - 1).astype(jnp.int32)
    n_tiles_used = (group_end[-1:] // tm).astype(jnp.int32)

    xs = _dispatch(xn, pos, n_tiles * tm)
    ys = _experts(xs, tile_expert, n_tiles_used, wg.astype(BF16), bg.reshape(N_EXPERTS, 1, D_MODEL),
                  wu.astype(BF16), bu.reshape(N_EXPERTS, 1, D_MODEL), wd.astype(BF16),
                  bd.reshape(N_EXPERTS, 1, D_MODEL))
    return _combine(ys, pos, gates, h)


def _pad_lanes(v, fill=0.0):
    return jnp.concatenate([v.astype(F32), jnp.full((LANES - v.shape[0],), fill, F32)]).reshape(1, LANES)


def kernel(x_prompt, x_sample, state_ssm, state_conv, cache_k_window, cache_v_window, mamba_norm_g, mamba_w_in, mamba_conv_w, mamba_conv_b, mamba_dt_bias, mamba_A_log, mamba_D, mamba_out_norm_g, mamba_w_out, kv_norm_g, w_kv, k_norm_g, attn_norm_g, w_q, q_norm_g, attn_sinks, w_o, rel_bias, moe_norm_g, moe_w_router, moe_b_router, moe_w_gate, moe_b_gate, moe_w_up, moe_b_up, moe_w_down, moe_b_down):
    x_all = jnp.concatenate([x_prompt.reshape(T_PROMPT, D_MODEL), x_sample.reshape(T_SAMPLE, D_MODEL)], axis=0)
    nchunk = SEQ // SSD_CHUNK

    w_in = jnp.pad(mamba_w_in[0], ((0, 0), (0, IN_PROJ_PAD - IN_PROJ_DIM))).astype(BF16)
    zx = _matmul(x_all, w_in, g=mamba_norm_g[0], tn=IN_PROJ_PAD // 7, name="mamba_in_proj")
    ssd_params = (mamba_conv_w[0], mamba_conv_b[0].reshape(1, CONV_DIM), _pad_lanes(mamba_dt_bias[0]),
                  _pad_lanes(mamba_A_log[0]), jnp.repeat(mamba_D[0], SSM_HEAD_DIM).reshape(1, D_INNER),
                  mamba_out_norm_g[0].reshape(1, D_INNER))
    y_buf = jnp.zeros((T_ALL, D_INNER), F32)
    y_buf, conv_p, ssm_p = _ssd(
        zx, y_buf, jnp.zeros((BATCH, CONV_WIDTH - 1, CONV_DIM), F32), jnp.zeros((BATCH, D_INNER, D_STATE), F32),
        ssd_params, nb=BATCH, nchunk=nchunk, rows=SSD_CHUNK, row_block0=0, name="ssd_prompt")
    y_buf, conv_s, ssm_s = _ssd(
        zx, y_buf, state_conv[0], state_ssm[0].reshape(DEC_BATCH, D_INNER, D_STATE),
        ssd_params, nb=DEC_BATCH, nchunk=1, rows=DEC_SEQ, row_block0=T_PROMPT // DEC_SEQ, name="ssd_sample")
    h = _matmul(y_buf, mamba_w_out[0].astype(BF16), res=x_all, tn=D_MODEL, name="mamba_out_proj")

    h = _moe(h, moe_norm_g[0], moe_w_router[0], moe_b_router[0], moe_w_gate[0], moe_b_gate[0],
             moe_w_up[0], moe_b_up[0], moe_w_down[0], moe_b_down[0])

    k_all, v_all = _shared_kv(h, kv_norm_g, w_kv, k_norm_g)

    q_all = _matmul(h, w_q[0].astype(BF16), g=attn_norm_g[0], tn=D_MODEL, name="attn_q_proj")
    sinks = jnp.broadcast_to(attn_sinks[0].astype(F32)[:, None], (N_HEADS, LANES))
    nblk = SEQ // WINDOW
    i = jnp.arange(WINDOW)[:, None]
    j = jnp.arange(2 * WINDOW)[None, :]
    dist = WINDOW + i - j
    in_window = (dist >= 0) & (dist < WINDOW)
    bias_p = _rel_bias(dist, rel_bias)
    bias_prompt = jnp.stack([jnp.where(in_window & (j >= WINDOW), bias_p, NEG_BIG),
                             jnp.where(in_window, bias_p, NEG_BIG)])
    i_s = jnp.arange(DEC_SEQ)[:, None]
    dist_s = WINDOW + i_s - j
    bias_sample = jnp.where((dist_s >= 0) & (dist_s < WINDOW), _rel_bias(dist_s, rel_bias), NEG_BIG)[None]

    a_buf = jnp.zeros((T_ALL, D_MODEL), F32)
    a_buf = _attention(
        q_all, k_all, k_all, v_all, v_all, bias_prompt, sinks, q_norm_g[0], a_buf,
        grid=(BATCH, nblk), lq=WINDOW, lc=WINDOW,
        q_map=lambda b, n: (b * nblk + n, 0),
        prev_map=lambda b, n: (b * nblk + jnp.maximum(n - 1, 0), 0),
        cur_map=lambda b, n: (b * nblk + n, 0),
        bias_map=lambda b, n: (jnp.minimum(n, 1), 0, 0, 0),
        name="attn_prompt")
    cache_k = cache_k_window.reshape(DEC_BATCH * WINDOW, KV_DIM).astype(F32)
    cache_v = cache_v_window.reshape(DEC_BATCH * WINDOW, KV_DIM).astype(F32)
    a_buf = _attention(
        q_all, cache_k, k_all, cache_v, v_all, bias_sample, sinks, q_norm_g[0], a_buf,
        grid=(DEC_BATCH,), lq=DEC_SEQ, lc=DEC_SEQ,
        q_map=lambda b: (T_PROMPT // DEC_SEQ + b, 0),
        prev_map=lambda b: (b, 0),
        cur_map=lambda b: (T_PROMPT // DEC_SEQ + b, 0),
        bias_map=lambda b: (0, 0, 0, 0),
        name="attn_sample")
    h = _matmul(a_buf, w_o[0].astype(BF16), res=h, tn=D_MODEL, name="attn_o_proj")

    h = _moe(h, moe_norm_g[1], moe_w_router[1], moe_b_router[1], moe_w_gate[1], moe_b_gate[1],
             moe_w_up[1], moe_b_up[1], moe_w_down[1], moe_b_down[1])

    y_prompt = h[:T_PROMPT].reshape(BATCH, SEQ, D_MODEL)
    y_sample = h[T_PROMPT:].reshape(DEC_BATCH, DEC_SEQ, D_MODEL)
    k_p = k_all[:T_PROMPT].reshape(BATCH, SEQ, N_KV_HEADS, HEAD_DIM)[:, -WINDOW:]
    v_p = v_all[:T_PROMPT].reshape(BATCH, SEQ, N_KV_HEADS, HEAD_DIM)[:, -WINDOW:]
    k_new = k_all[T_PROMPT:].reshape(DEC_BATCH, DEC_SEQ, N_KV_HEADS, HEAD_DIM)
    v_new = v_all[T_PROMPT:].reshape(DEC_BATCH, DEC_SEQ, N_KV_HEADS, HEAD_DIM)
    k_s = jnp.concatenate([cache_k_window.astype(F32), k_new], axis=1)[:, -WINDOW:]
    v_s = jnp.concatenate([cache_v_window.astype(F32), v_new], axis=1)[:, -WINDOW:]
    ssm_shape = (1, -1, N_SSM_HEADS, SSM_HEAD_DIM, D_STATE)
    return (y_prompt, y_sample, ssm_p.reshape(ssm_shape), conv_p[None], k_p, v_p,
            ssm_s.reshape(ssm_shape), conv_s[None], k_s, v_s)
```

```python
import functools
import math

import jax
import jax.numpy as jnp
from jax import lax
from jax.experimental import pallas as pl
from jax.experimental.pallas import tpu as pltpu

F32 = jnp.float32
BF16 = jnp.bfloat16
HIGHEST = lax.Precision.HIGHEST

D_MODEL = 1024
BATCH = 8
SEQ = 2048
DEC_BATCH = 128
DEC_SEQ = 8
T_PROMPT = BATCH * SEQ
T_SAMPLE = DEC_BATCH * DEC_SEQ
T_ALL = T_PROMPT + T_SAMPLE

D_INNER = 2048
SSM_HEAD_DIM = 64
N_SSM_HEADS = 32
D_STATE = 128
N_SSM_GROUPS = 8
HEADS_PER_GROUP = N_SSM_HEADS // N_SSM_GROUPS
CONV_WIDTH = 4
CONV_DIM = D_INNER + 2 * N_SSM_GROUPS * D_STATE
IN_PROJ_DIM = D_INNER + CONV_DIM + N_SSM_HEADS
IN_PROJ_PAD = 6272
SSD_CHUNK = 128

HEAD_DIM = 64
N_HEADS = 16
N_KV_HEADS = 4
Q_PER_KV = 4
KV_DIM = N_KV_HEADS * HEAD_DIM
WINDOW = 128
REL_BUCKETS = 32
REL_MAX_DIST = 128

N_EXPERTS = 32
TOP_K = 4
SWIGLU_LIMIT = 7.0
SWIGLU_ALPHA = 1.702

NORM_EPS = 1e-6
NEG_BIG = -1e30

LANES = 128
TOKEN_TILE = 512
EXPERT_TILE = 512
ROW_DMA_TILE = 256
VMEM_LIMIT = 56 * 1024 * 1024


def _cparams(sem):
    return pltpu.CompilerParams(dimension_semantics=sem, vmem_limit_bytes=VMEM_LIMIT)


def _rms(x, g):
    return x * lax.rsqrt(jnp.mean(x * x, axis=-1, keepdims=True) + NORM_EPS) * g


def _sigmoid(x):
    return 1.0 / (1.0 + jnp.exp(-x))


def _segment_map(lo, ntiles, with_j):
    if with_j:
        return lambda i, j: (jnp.clip(i - lo, 0, ntiles - 1), j)
    return lambda i, j: (jnp.clip(i - lo, 0, ntiles - 1), 0)


def _matmul_kernel(*refs, bounds, res_bounds, norm):
    refs = list(refs)
    x_refs = [refs.pop(0) for _ in range(len(bounds) - 1)]
    g_ref = refs.pop(0) if norm else None
    w_ref = refs.pop(0)
    r_refs = [refs.pop(0) for _ in range(len(res_bounds) - 1)]
    o_ref, xs_ref = refs
    i = pl.program_id(0)

    for s, x_ref in enumerate(x_refs):
        @pl.when((pl.program_id(1) == 0) & (i >= bounds[s]) & (i < bounds[s + 1]))
        def _(x_ref=x_ref):
            x = x_ref[...].astype(F32)
            if norm:
                x = _rms(x, g_ref[...])
            xs_ref[...] = x.astype(BF16)

    acc = jnp.dot(xs_ref[...], w_ref[...], preferred_element_type=F32)
    if r_refs:
        for s, r_ref in enumerate(r_refs):
            @pl.when((i >= res_bounds[s]) & (i < res_bounds[s + 1]))
            def _(r_ref=r_ref):
                o_ref[...] = acc + r_ref[...]
    else:
        o_ref[...] = acc


def _tile_bounds(segs, tm):
    bounds = [0]
    for x in segs:
        bounds.append(bounds[-1] + x.shape[0] // tm)
    return tuple(bounds)


def _matmul(xs, w, *, g=None, res=None, tn, name):
    k = xs[0].shape[1]
    n = w.shape[1]
    tm = TOKEN_TILE
    res = res or []
    bounds = _tile_bounds(xs, tm)
    res_bounds = _tile_bounds(res, tm)
    in_specs = [pl.BlockSpec((tm, k), _segment_map(bounds[s], bounds[s + 1] - bounds[s], False))
                for s in range(len(xs))]
    args = list(xs)
    if g is not None:
        in_specs.append(pl.BlockSpec((1, k), lambda i, j: (0, 0)))
        args.append(g.reshape(1, k).astype(F32))
    in_specs.append(pl.BlockSpec((k, tn), lambda i, j: (0, j)))
    args.append(w)
    in_specs += [pl.BlockSpec((tm, tn), _segment_map(res_bounds[s], res_bounds[s + 1] - res_bounds[s], True))
                 for s in range(len(res))]
    args += list(res)
    return pl.pallas_call(
        functools.partial(_matmul_kernel, bounds=bounds, res_bounds=res_bounds, norm=g is not None),
        out_shape=jax.ShapeDtypeStruct((bounds[-1] * tm, n), F32),
        grid=(bounds[-1], n // tn),
        in_specs=in_specs,
        out_specs=pl.BlockSpec((tm, tn), lambda i, j: (i, j)),
        scratch_shapes=[pltpu.VMEM((tm, k), BF16)],
        compiler_params=_cparams(("parallel", "arbitrary")),
        name=name,
    )(*args)


def _ssd_kernel(z_ref, x_ref, bc_ref, dt_ref, cw_ref, cb_ref, dtb_ref, alog_ref, dexp_ref, ng_ref,
                cinit_ref, hinit_ref, y_ref, cstate_ref, hstate_ref,
                xpad_ref, act_ref, h_ref, *, rows):
    q = SSD_CHUNK
    c = pl.program_id(1)
    pad = 8

    @pl.when(c == 0)
    def _():
        xpad_ref[pad - 3:pad, :] = cinit_ref[0]
        h_ref[...] = hinit_ref[0]

    xpad_ref[pad:pad + rows, 0:D_INNER] = x_ref[...]
    xpad_ref[pad:pad + rows, D_INNER:CONV_DIM] = bc_ref[...]
    if rows < q:
        xpad_ref[pad + rows:pad + q, :] = jnp.zeros((q - rows, CONV_DIM), F32)

    slab = 512
    for s in range(CONV_DIM // slab):
        cs = slice(s * slab, (s + 1) * slab)
        acc = cb_ref[:, cs]
        for k in range(CONV_WIDTH):
            acc = acc + xpad_ref[pad - 3 + k:pad - 3 + k + q, cs] * cw_ref[k:k + 1, cs]
        act_ref[:, cs] = acc * _sigmoid(acc)

    tail = xpad_ref[pad + rows - 3:pad + rows, :]
    cstate_ref[0] = tail
    xpad_ref[pad - 3:pad, :] = tail

    dt_raw = dt_ref[...]
    if rows < q:
        dt_raw = jnp.concatenate([dt_raw, jnp.zeros((q - rows, LANES), F32)], axis=0)
    xb = dt_raw + dtb_ref[...]
    dt = jnp.maximum(xb, 0.0) + jnp.log1p(jnp.exp(-jnp.abs(xb)))
    row_id = lax.broadcasted_iota(jnp.int32, (q, LANES), 0)
    if rows < q:
        dt = jnp.where(row_id < rows, dt, 0.0)
    a = dt * (-jnp.exp(alog_ref[...]))
    col_id = lax.broadcasted_iota(jnp.int32, (q, q), 1)
    row_q = lax.broadcasted_iota(jnp.int32, (q, q), 0)
    causal = row_q >= col_id
    a_cum = jnp.dot(causal.astype(F32), a, precision=HIGHEST, preferred_element_type=F32)
    a_cum_t = a_cum.T
    dt_t = dt.T
    w_end_t = (jnp.exp(a_cum[q - 1:q, :] - a_cum) * dt).T
    chunk_decay = jnp.broadcast_to(jnp.exp(a_cum_t[:, q - 1:q]), (LANES, D_STATE))
    lane_lo = lax.broadcasted_iota(jnp.int32, (q, LANES), 1) < SSM_HEAD_DIM

    gw = HEADS_PER_GROUP * SSM_HEAD_DIM
    for g in range(N_SSM_GROUPS):
        b_g = act_ref[:, D_INNER + g * D_STATE:D_INNER + (g + 1) * D_STATE].astype(BF16)
        c_g = act_ref[:, D_INNER + N_SSM_GROUPS * D_STATE + g * D_STATE:
                      D_INNER + N_SSM_GROUPS * D_STATE + (g + 1) * D_STATE].astype(BF16)
        x_g = act_ref[:, g * gw:(g + 1) * gw]
        h_prev = h_ref[g * gw:(g + 1) * gw, :]
        cb = lax.dot_general(c_g, b_g, (((1,), (1,)), ((), ())), preferred_element_type=F32)
        y_off = lax.dot_general(c_g, h_prev.astype(BF16), (((1,), (1,)), ((), ())),
                                preferred_element_type=F32)
        y_pairs = []
        for pr in range(HEADS_PER_GROUP // 2):
            x_pair = x_g[:, pr * LANES:(pr + 1) * LANES]
            x_pair_bf = x_pair.astype(BF16)
            halves = []
            for half in range(2):
                h = g * HEADS_PER_GROUP + pr * 2 + half
                a_col = jnp.broadcast_to(a_cum[:, h:h + 1], (q, q))
                decay = jnp.exp(jnp.where(causal, a_col - a_cum_t[h:h + 1, :], -jnp.inf))
                m = (cb * decay * dt_t[h:h + 1, :]).astype(BF16)
                y_d = jnp.dot(m, x_pair_bf, preferred_element_type=F32)
                halves.append(y_d + y_off[:, pr * LANES:(pr + 1) * LANES] * jnp.exp(a_col))
            y_pair = jnp.where(lane_lo, halves[0], halves[1])
            y_pairs.append(y_pair + dexp_ref[:, g * gw + pr * LANES:g * gw + (pr + 1) * LANES] * x_pair)
        y_g = jnp.concatenate(y_pairs, axis=1)

        x_t = x_g.T
        xw_t = jnp.concatenate(
            [x_t[r * SSM_HEAD_DIM:(r + 1) * SSM_HEAD_DIM, :] *
             w_end_t[g * HEADS_PER_GROUP + r:g * HEADS_PER_GROUP + r + 1, :] for r in range(HEADS_PER_GROUP)],
            axis=0).astype(BF16)
        cd_g = jnp.concatenate(
            [jnp.broadcast_to(chunk_decay[g * HEADS_PER_GROUP + r:g * HEADS_PER_GROUP + r + 1, :],
                              (SSM_HEAD_DIM, D_STATE)) for r in range(HEADS_PER_GROUP)], axis=0)
        h_ref[g * gw:(g + 1) * gw, :] = cd_g * h_prev + jnp.dot(xw_t, b_g, preferred_element_type=F32)

        z_g = z_ref[:, g * gw:(g + 1) * gw]
        if rows < q:
            y_g = y_g[:rows]
        y_g = y_g * (z_g * _sigmoid(z_g))
        y_g = _rms(y_g, ng_ref[:, g * gw:(g + 1) * gw])
        y_ref[:, g * gw:(g + 1) * gw] = y_g.astype(y_ref.dtype)

    hstate_ref[0] = h_ref[...]


def _ssd(zx, conv_init, h_init, params, *, nb, nchunk, rows, row_block0, name):
    cw, cb, dtb, alog, dexp, ng = params
    rb = lambda b, c: row_block0 + b * nchunk + c
    in_specs = [
        pl.BlockSpec((rows, D_INNER), lambda b, c: (rb(b, c), 0)),
        pl.BlockSpec((rows, D_INNER), lambda b, c: (rb(b, c), 1)),
        pl.BlockSpec((rows, D_INNER), lambda b, c: (rb(b, c), 2)),
        pl.BlockSpec((rows, LANES), lambda b, c: (rb(b, c), (D_INNER + CONV_DIM) // LANES)),
        pl.BlockSpec((CONV_WIDTH, CONV_DIM), lambda b, c: (0, 0)),
        pl.BlockSpec((1, CONV_DIM), lambda b, c: (0, 0)),
        pl.BlockSpec((1, LANES), lambda b, c: (0, 0)),
        pl.BlockSpec((1, LANES), lambda b, c: (0, 0)),
        pl.BlockSpec((1, D_INNER), lambda b, c: (0, 0)),
        pl.BlockSpec((1, D_INNER), lambda b, c: (0, 0)),
        pl.BlockSpec((1, CONV_WIDTH - 1, CONV_DIM), lambda b, c: (b, 0, 0)),
        pl.BlockSpec((1, D_INNER, D_STATE), lambda b, c: (b, 0, 0)),
    ]
    out_shape = (
        jax.ShapeDtypeStruct((nb * nchunk * rows, D_INNER), F32),
        jax.ShapeDtypeStruct((nb, CONV_WIDTH - 1, CONV_DIM), F32),
        jax.ShapeDtypeStruct((nb, D_INNER, D_STATE), F32),
    )
    out_specs = (
        pl.BlockSpec((rows, D_INNER), lambda b, c: (b * nchunk + c, 0)),
        pl.BlockSpec((1, CONV_WIDTH - 1, CONV_DIM), lambda b, c: (b, 0, 0)),
        pl.BlockSpec((1, D_INNER, D_STATE), lambda b, c: (b, 0, 0)),
    )
    return pl.pallas_call(
        functools.partial(_ssd_kernel, rows=rows),
        out_shape=out_shape,
        grid=(nb, nchunk),
        in_specs=in_specs,
        out_specs=out_specs,
        scratch_shapes=[pltpu.VMEM((SSD_CHUNK + 8, CONV_DIM), F32),
                        pltpu.VMEM((SSD_CHUNK, CONV_DIM), F32),
                        pltpu.VMEM((D_INNER, D_STATE), F32)],
        compiler_params=_cparams(("parallel", "arbitrary")),
        name=name,
    )(zx, zx, zx, zx, cw, cb, dtb, alog, dexp, ng, conv_init, h_init)


def _kv_kernel(x_ref, g_ref, w_ref, kg_ref, bd_ref, k_ref, v_ref):
    xn = _rms(x_ref[...], g_ref[...]).astype(BF16)
    kv = jnp.dot(xn, w_ref[...], preferred_element_type=F32)
    k = kv[:, :KV_DIM]
    ms = jnp.dot(k * k, bd_ref[...], precision=HIGHEST, preferred_element_type=F32)
    k_ref[...] = k * lax.rsqrt(ms + NORM_EPS) * kg_ref[...]
    v_ref[...] = kv[:, KV_DIM:]


def _shared_kv(h, g, w_kv, k_norm_g):
    m = h.shape[0]
    tm = TOKEN_TILE
    head_of = jnp.arange(KV_DIM) // HEAD_DIM
    block_mean = (head_of[:, None] == head_of[None, :]).astype(F32) / HEAD_DIM
    return pl.pallas_call(
        _kv_kernel,
        out_shape=(jax.ShapeDtypeStruct((m, KV_DIM), F32), jax.ShapeDtypeStruct((m, KV_DIM), F32)),
        grid=(m // tm,),
        in_specs=[pl.BlockSpec((tm, D_MODEL), lambda i: (i, 0)),
                  pl.BlockSpec((1, D_MODEL), lambda i: (0, 0)),
                  pl.BlockSpec((D_MODEL, 2 * KV_DIM), lambda i: (0, 0)),
                  pl.BlockSpec((1, KV_DIM), lambda i: (0, 0)),
                  pl.BlockSpec((KV_DIM, KV_DIM), lambda i: (0, 0))],
        out_specs=(pl.BlockSpec((tm, KV_DIM), lambda i: (i, 0)), pl.BlockSpec((tm, KV_DIM), lambda i: (i, 0))),
        compiler_params=_cparams(("parallel",)),
        name="shared_kv",
    )(h, g.reshape(1, D_MODEL), w_kv.astype(BF16), jnp.tile(k_norm_g, N_KV_HEADS).reshape(1, KV_DIM), block_mean)


def _attn_kernel(q_ref, kp_ref, kc_ref, vp_ref, vc_ref, bias_ref, sink_ref, qg_ref, o_ref,
                 kband_ref, vband_ref, *, lq, lc):
    kband_ref[0:WINDOW, :] = kp_ref[...]
    vband_ref[0:WINDOW, :] = vp_ref[...]
    kband_ref[WINDOW:WINDOW + lc, :] = kc_ref[...]
    vband_ref[WINDOW:WINDOW + lc, :] = vc_ref[...]
    if lc < WINDOW:
        kband_ref[WINDOW + lc:, :] = jnp.zeros((WINDOW - lc, KV_DIM), F32)
        vband_ref[WINDOW + lc:, :] = jnp.zeros((WINDOW - lc, KV_DIM), F32)

    outs = []
    for g in range(N_KV_HEADS):
        k_g = kband_ref[:, g * HEAD_DIM:(g + 1) * HEAD_DIM].astype(BF16)
        v_g = vband_ref[:, g * HEAD_DIM:(g + 1) * HEAD_DIM].astype(BF16)
        qs, sinks = [], []
        for r in range(Q_PER_KV):
            h = g * Q_PER_KV + r
            qs.append(_rms(q_ref[:, h * HEAD_DIM:(h + 1) * HEAD_DIM], qg_ref[...]))
            sinks.append(jnp.broadcast_to(sink_ref[h:h + 1, 0:1], (lq, 1)))
        q_g = jnp.concatenate(qs, axis=0).astype(BF16)
        sink = jnp.concatenate(sinks, axis=0)
        s = lax.dot_general(q_g, k_g, (((1,), (1,)), ((), ())), preferred_element_type=F32)
        s = s * (HEAD_DIM ** -0.5) + bias_ref[0, g * Q_PER_KV:(g + 1) * Q_PER_KV].reshape(Q_PER_KV * lq, 2 * WINDOW)
        m = jnp.maximum(jnp.max(s, axis=-1, keepdims=True), sink)
        p = jnp.exp(s - m)
        den = jnp.sum(p, axis=-1, keepdims=True) + jnp.exp(sink - m)
        o = jnp.dot(p.astype(BF16), v_g, preferred_element_type=F32) / den
        for r in range(Q_PER_KV):
            outs.append(o[r * lq:(r + 1) * lq])
    o_ref[...] = jnp.concatenate(outs, axis=1).astype(o_ref.dtype)


def _attention(q_all, k_prev, k_cur, v_prev, v_cur, bias, sinks, q_norm_g, *,
               grid, lq, lc, q_map, prev_map, cur_map, bias_map, out_rows, out_map, name):
    in_specs = [
        pl.BlockSpec((lq, D_MODEL), q_map),
        pl.BlockSpec((WINDOW, KV_DIM), prev_map),
        pl.BlockSpec((lc, KV_DIM), cur_map),
        pl.BlockSpec((WINDOW, KV_DIM), prev_map),
        pl.BlockSpec((lc, KV_DIM), cur_map),
        pl.BlockSpec((1, N_HEADS, lq, 2 * WINDOW), bias_map),
        pl.BlockSpec((N_HEADS, LANES), lambda *_: (0, 0)),
        pl.BlockSpec((1, HEAD_DIM), lambda *_: (0, 0)),
    ]
    return pl.pallas_call(
        functools.partial(_attn_kernel, lq=lq, lc=lc),
        out_shape=jax.ShapeDtypeStruct((out_rows, D_MODEL), F32),
        grid=grid,
        in_specs=in_specs,
        out_specs=pl.BlockSpec((lq, D_MODEL), out_map),
        scratch_shapes=[pltpu.VMEM((2 * WINDOW, KV_DIM), F32), pltpu.VMEM((2 * WINDOW, KV_DIM), F32)],
        compiler_params=_cparams(("parallel",) * len(grid)),
        name=name,
    )(q_all, k_prev, k_cur, v_prev, v_cur, bias, sinks, q_norm_g.reshape(1, HEAD_DIM))


def _rel_bias(dist, table):
    n = jnp.maximum(dist, 0)
    max_exact = REL_BUCKETS // 2
    nf = jnp.maximum(n, 1).astype(F32)
    large = max_exact + (jnp.log(nf / max_exact) / math.log(REL_MAX_DIST / max_exact)
                         * (REL_BUCKETS - max_exact)).astype(jnp.int32)
    large = jnp.minimum(large, REL_BUCKETS - 1)
    bucket = jnp.where(n < max_exact, n, large)
    onehot = (bucket[None, :, :] == jnp.arange(REL_BUCKETS)[:, None, None]).astype(F32)
    return jnp.einsum("bh,bqk->hqk", table.astype(F32), onehot, precision=HIGHEST)


def _router_kernel(h_ref, g_ref, wr_ref, br_ref, xn_ref, idx_ref, gate_ref):
    xn = _rms(h_ref[...], g_ref[...])
    xn_ref[...] = xn
    logits = jnp.dot(xn, wr_ref[...], precision=HIGHEST, preferred_element_type=F32) + br_ref[...]
    lane = lax.broadcasted_iota(jnp.int32, logits.shape, 1)
    vals, idxs = [], []
    for _ in range(TOP_K):
        m = jnp.max(logits, axis=-1, keepdims=True)
        idx = jnp.min(jnp.where(logits == m, lane, LANES), axis=-1, keepdims=True)
        logits = jnp.where(lane == idx, -jnp.inf, logits)
        vals.append(m)
        idxs.append(idx)
    es = [jnp.exp(v - vals[0]) for v in vals]
    den = es[0] + es[1] + es[2] + es[3]
    idx_out = jnp.zeros(logits.shape, jnp.int32)
    gate_out = jnp.zeros(logits.shape, F32)
    for k in range(TOP_K):
        idx_out = jnp.where(lane == k, idxs[k], idx_out)
        gate_out = jnp.where(lane == k, es[k] / den, gate_out)
    idx_ref[...] = idx_out
    gate_ref[...] = gate_out


def _router(h, g, w_router, b_router):
    m = h.shape[0]
    tm = TOKEN_TILE
    wr = jnp.pad(w_router, ((0, 0), (0, LANES - N_EXPERTS)))
    br = jnp.concatenate([b_router, jnp.full((LANES - N_EXPERTS,), NEG_BIG, F32)]).reshape(1, LANES)
    row = lambda i: (i, 0)
    fix = lambda i: (0, 0)
    return pl.pallas_call(
        _router_kernel,
        out_shape=(jax.ShapeDtypeStruct((m, D_MODEL), F32),
                   jax.ShapeDtypeStruct((m, LANES), jnp.int32),
                   jax.ShapeDtypeStruct((m, LANES), F32)),
        grid=(m // tm,),
        in_specs=[pl.BlockSpec((tm, D_MODEL), row), pl.BlockSpec((1, D_MODEL), fix),
                  pl.BlockSpec((D_MODEL, LANES), fix), pl.BlockSpec((1, LANES), fix)],
        out_specs=(pl.BlockSpec((tm, D_MODEL), row), pl.BlockSpec((tm, LANES), row), pl.BlockSpec((tm, LANES), row)),
        compiler_params=_cparams(("parallel",)),
        name="moe_router",
    )(h, g.reshape(1, D_MODEL), wr, br)


def _row_copy(src_hbm, src_row, dst, dst_row, sem):
    return pltpu.make_async_copy(src_hbm.at[pl.ds(src_row, 1)], dst.at[pl.ds(dst_row, 1)], sem)


PAD_CHUNKS = (256, 128, 64, 32, 16, 8)


def _pad_fill(pad_start_ref, pad_len_ref, zero_ref, xs_hbm, sem, wait):
    def go(copy):
        copy.wait() if wait else copy.start()

    def per_expert(e, carry):
        start = pad_start_ref[e]
        n = pad_len_ref[e]
        head = jnp.minimum((-start) & 7, n)
        for r in range(7):
            @pl.when(r < head)
            def _(r=r):
                go(pltpu.make_async_copy(zero_ref.at[pl.ds(0, 1)], xs_hbm.at[pl.ds(start + r, 1)], sem))
        body = n - head
        base = start + head
        for c in PAD_CHUNKS:
            @pl.when((body & c) != 0)
            def _(c=c):
                off = pl.multiple_of(base + (body & ~(2 * c - 1)), 8)
                go(pltpu.make_async_copy(zero_ref.at[pl.ds(0, c)], xs_hbm.at[pl.ds(off, c)], sem))
        tail = body & 7
        tail_base = base + (body & ~7)
        for r in range(7):
            @pl.when(r < tail)
            def _(r=r):
                go(pltpu.make_async_copy(zero_ref.at[pl.ds(0, 1)], xs_hbm.at[pl.ds(tail_base + r, 1)], sem))
        return carry

    lax.fori_loop(0, N_EXPERTS, per_expert, 0)

    c = PAD_CHUNKS[0]
    tail_start = pad_start_ref[N_EXPERTS]

    def per_chunk(t, carry):
        off = pl.multiple_of(tail_start + t * c, c)
        go(pltpu.make_async_copy(zero_ref, xs_hbm.at[pl.ds(off, c)], sem))
        return carry

    lax.fori_loop(0, (xs_hbm.shape[0] - tail_start) // c, per_chunk, 0)


def _dispatch_kernel(pad_start_ref, pad_len_ref, pos_ref, x_ref, xs_hbm, zero_ref, sems):
    @pl.when(pl.program_id(0) == 0)
    def _():
        zero_ref[...] = jnp.zeros(zero_ref.shape, zero_ref.dtype)
        _pad_fill(pad_start_ref, pad_len_ref, zero_ref, xs_hbm, sems.at[1], wait=False)

    def issue(j, carry):
        for k in range(TOP_K):
            _row_copy(x_ref, j, xs_hbm, pos_ref[j * TOP_K + k], sems.at[0]).start()
        return carry

    def drain(j, carry):
        for k in range(TOP_K):
            _row_copy(x_ref, j, xs_hbm, pos_ref[j * TOP_K + k], sems.at[0]).wait()
        return carry

    lax.fori_loop(0, ROW_DMA_TILE, issue, 0)
    lax.fori_loop(0, ROW_DMA_TILE, drain, 0)

    @pl.when(pl.program_id(0) == 0)
    def _():
        _pad_fill(pad_start_ref, pad_len_ref, zero_ref, xs_hbm, sems.at[1], wait=True)


def _dispatch(xn, pos_flat, pad_start, pad_len, n_rows):
    m = xn.shape[0]
    return pl.pallas_call(
        _dispatch_kernel,
        out_shape=jax.ShapeDtypeStruct((n_rows, D_MODEL), xn.dtype),
        grid_spec=pltpu.PrefetchScalarGridSpec(
            num_scalar_prefetch=2,
            grid=(m // ROW_DMA_TILE,),
            in_specs=[pl.BlockSpec((ROW_DMA_TILE * TOP_K,), lambda i, ps, pn: (i,), memory_space=pltpu.SMEM),
                      pl.BlockSpec((ROW_DMA_TILE, D_MODEL), lambda i, ps, pn: (i, 0))],
            out_specs=pl.BlockSpec(memory_space=pl.ANY),
            scratch_shapes=[pltpu.VMEM((PAD_CHUNKS[0], D_MODEL), xn.dtype), pltpu.SemaphoreType.DMA((2,))]),
        compiler_params=_cparams(("arbitrary",)),
        name="moe_dispatch",
    )(pad_start, pad_len, pos_flat, xn)


def _expert_kernel(te_ref, nt_ref, x_ref, wg_ref, bg_ref, wu_ref, bu_ref, wd_ref, bd_ref, y_ref,
                   wg_bf, wu_bf, wd_bf):
    i = pl.program_id(0)
    used = i < nt_ref[0]

    @pl.when(used & ((i == 0) | (te_ref[i] != te_ref[jnp.maximum(i - 1, 0)])))
    def _():
        wg_bf[...] = wg_ref[0].astype(BF16)
        wu_bf[...] = wu_ref[0].astype(BF16)
        wd_bf[...] = wd_ref[0].astype(BF16)

    @pl.when(used)
    def _():
        x = x_ref[...].astype(BF16)
        gt = jnp.minimum(jnp.dot(x, wg_bf[...], preferred_element_type=F32) + bg_ref[0], SWIGLU_LIMIT)
        up = jnp.clip(jnp.dot(x, wu_bf[...], preferred_element_type=F32) + bu_ref[0], -SWIGLU_LIMIT, SWIGLU_LIMIT)
        mid = gt * _sigmoid(SWIGLU_ALPHA * gt) * (up + 1.0)
        y_ref[...] = jnp.dot(mid.astype(BF16), wd_bf[...], preferred_element_type=F32) + bd_ref[0]

    @pl.when(jnp.logical_not(used))
    def _():
        y_ref[...] = jnp.zeros(y_ref.shape, y_ref.dtype)


def _experts(xs, tile_expert, n_tiles_used, layer, wg, bg, wu, bu, wd, bd):
    n_rows = xs.shape[0]
    tm = EXPERT_TILE
    wspec = pl.BlockSpec((None, 1, D_MODEL, D_MODEL), lambda i, te, nt: (layer, te[i], 0, 0))
    bspec = pl.BlockSpec((None, 1, 1, D_MODEL), lambda i, te, nt: (layer, te[i], 0, 0))
    xspec = pl.BlockSpec((tm, D_MODEL), lambda i, te, nt: (jnp.minimum(i, nt[0] - 1), 0))
    return pl.pallas_call(
        _expert_kernel,
        out_shape=jax.ShapeDtypeStruct((n_rows, D_MODEL), F32),
        grid_spec=pltpu.PrefetchScalarGridSpec(
            num_scalar_prefetch=2,
            grid=(n_rows // tm,),
            in_specs=[xspec, wspec, bspec, wspec, bspec, wspec, bspec],
            out_specs=pl.BlockSpec((tm, D_MODEL), lambda i, te, nt: (i, 0)),
            scratch_shapes=[pltpu.VMEM((D_MODEL, D_MODEL), BF16)] * 3),
        compiler_params=_cparams(("arbitrary",)),
        name="moe_experts",
    )(tile_expert, n_tiles_used, xs, wg, bg, wu, bu, wd, bd)


def _combine_kernel(pos_ref, ys_hbm, gate_ref, res_ref, *rest, split_tile):
    *o_refs, buf_ref, sem = rest

    def issue(j, carry):
        for k in range(TOP_K):
            _row_copy(ys_hbm, pos_ref[j * TOP_K + k], buf_ref.at[k], j, sem).start()
        return carry

    def drain(j, carry):
        for k in range(TOP_K):
            _row_copy(ys_hbm, pos_ref[j * TOP_K + k], buf_ref.at[k], j, sem).wait()
        return carry

    lax.fori_loop(0, ROW_DMA_TILE, issue, 0)
    lax.fori_loop(0, ROW_DMA_TILE, drain, 0)
    acc = res_ref[...]
    for k in range(TOP_K):
        acc = acc + gate_ref[:, k:k + 1] * buf_ref[k]
    if split_tile is None:
        o_refs[0][...] = acc
    else:
        @pl.when(pl.program_id(0) < split_tile)
        def _():
            o_refs[0][...] = acc

        @pl.when(pl.program_id(0) >= split_tile)
        def _():
            o_refs[1][...] = acc


def _combine(ys, pos_flat, gates, res, split_rows=None):
    m = res.shape[0]
    tm = ROW_DMA_TILE
    if split_rows is None:
        split_tile = None
        out_shape = jax.ShapeDtypeStruct((m, D_MODEL), F32)
        out_specs = pl.BlockSpec((tm, D_MODEL), lambda i: (i, 0))
    else:
        split_tile = split_rows // tm
        out_shape = (jax.ShapeDtypeStruct((split_rows, D_MODEL), F32),
                     jax.ShapeDtypeStruct((m - split_rows, D_MODEL), F32))
        out_specs = (pl.BlockSpec((tm, D_MODEL), lambda i: (jnp.minimum(i, split_tile - 1), 0)),
                     pl.BlockSpec((tm, D_MODEL), lambda i: (jnp.maximum(i - split_tile, 0), 0)))
    return pl.pallas_call(
        functools.partial(_combine_kernel, split_tile=split_tile),
        out_shape=out_shape,
        grid=(m // tm,),
        in_specs=[pl.BlockSpec((tm * TOP_K,), lambda i: (i,), memory_space=pltpu.SMEM),
                  pl.BlockSpec(memory_space=pl.ANY),
                  pl.BlockSpec((tm, LANES), lambda i: (i, 0)),
                  pl.BlockSpec((tm, D_MODEL), lambda i: (i, 0))],
        out_specs=out_specs,
        scratch_shapes=[pltpu.VMEM((TOP_K, tm, D_MODEL), F32), pltpu.SemaphoreType.DMA],
        compiler_params=_cparams(("arbitrary",)),
        name="moe_combine",
    )(pos_flat, ys, gates, res)


def _moe(h, g, w_router, b_router, layer, wg, bg, wu, bu, wd, bd, split_rows=None):
    m = h.shape[0]
    tm = EXPERT_TILE
    n_tiles = m * TOP_K // tm + N_EXPERTS
    xn, idx, gates = _router(h, g, w_router, b_router)

    top_i = idx[:, :TOP_K]
    onehot = (top_i[:, :, None] == jnp.arange(N_EXPERTS, dtype=jnp.int32)).astype(jnp.int32)
    sel = jnp.sum(onehot, axis=1)
    csum = jnp.cumsum(sel, axis=0)
    counts = csum[-1]
    padded = (counts + tm - 1) // tm * tm
    group_end = jnp.cumsum(padded)
    slot = (group_end - padded)[None, :] + csum - sel
    pos = jnp.sum(onehot * slot[:, None, :], axis=-1).astype(jnp.int32).reshape(-1)
    tile_start = jnp.arange(n_tiles, dtype=jnp.int32) * tm
    tile_expert = jnp.minimum(jnp.sum((tile_start[:, None] >= group_end[None, :]).astype(jnp.int32), axis=1),
                              N_EXPERTS - 1)
    n_tiles_used = (group_end[-1:] // tm).astype(jnp.int32)
    pad_start = jnp.concatenate([group_end - padded + counts, group_end[-1:]]).astype(jnp.int32)
    pad_len = (padded - counts).astype(jnp.int32)

    xs = _dispatch(xn, pos, pad_start, pad_len, n_tiles * tm)
    ys = _experts(xs, tile_expert, n_tiles_used, layer, wg, bg.reshape(-1, N_EXPERTS, 1, D_MODEL),
                  wu, bu.reshape(-1, N_EXPERTS, 1, D_MODEL), wd, bd.reshape(-1, N_EXPERTS, 1, D_MODEL))
    return _combine(ys, pos, gates, h, split_rows)


def _pad_lanes(v, fill=0.0):
    return jnp.concatenate([v.astype(F32), jnp.full((LANES - v.shape[0],), fill, F32)]).reshape(1, LANES)


def kernel(x_prompt, x_sample, state_ssm, state_conv, cache_k_window, cache_v_window, mamba_norm_g, mamba_w_in, mamba_conv_w, mamba_conv_b, mamba_dt_bias, mamba_A_log, mamba_D, mamba_out_norm_g, mamba_w_out, kv_norm_g, w_kv, k_norm_g, attn_norm_g, w_q, q_norm_g, attn_sinks, w_o, rel_bias, moe_norm_g, moe_w_router, moe_b_router, moe_w_gate, moe_b_gate, moe_w_up, moe_b_up, moe_w_down, moe_b_down):
    x_segs = [x_prompt.reshape(T_PROMPT, D_MODEL), x_sample.reshape(T_SAMPLE, D_MODEL)]
    nchunk = SEQ // SSD_CHUNK

    w_in = jnp.pad(mamba_w_in[0], ((0, 0), (0, IN_PROJ_PAD - IN_PROJ_DIM))).astype(BF16)
    zx = _matmul(x_segs, w_in, g=mamba_norm_g[0], tn=IN_PROJ_PAD // 7, name="mamba_in_proj")
    ssd_params = (mamba_conv_w[0], mamba_conv_b[0].reshape(1, CONV_DIM), _pad_lanes(mamba_dt_bias[0]),
                  _pad_lanes(mamba_A_log[0]), jnp.repeat(mamba_D[0], SSM_HEAD_DIM).reshape(1, D_INNER),
                  mamba_out_norm_g[0].reshape(1, D_INNER))
    y_p, conv_p, ssm_p = _ssd(
        zx, jnp.zeros((BATCH, CONV_WIDTH - 1, CONV_DIM), F32), jnp.zeros((BATCH, D_INNER, D_STATE), F32),
        ssd_params, nb=BATCH, nchunk=nchunk, rows=SSD_CHUNK, row_block0=0, name="ssd_prompt")
    y_s, conv_s, ssm_s = _ssd(
        zx, state_conv[0], state_ssm[0].reshape(DEC_BATCH, D_INNER, D_STATE),
        ssd_params, nb=DEC_BATCH, nchunk=1, rows=DEC_SEQ, row_block0=T_PROMPT // DEC_SEQ, name="ssd_sample")
    h = _matmul([y_p, y_s], mamba_w_out[0].astype(BF16), res=x_segs, tn=D_MODEL, name="mamba_out_proj")

    h = _moe(h, moe_norm_g[0], moe_w_router[0], moe_b_router[0], 0, moe_w_gate, moe_b_gate,
             moe_w_up, moe_b_up, moe_w_down, moe_b_down)

    k_all, v_all = _shared_kv(h, kv_norm_g, w_kv, k_norm_g)

    q_all = _matmul([h], w_q[0].astype(BF16), g=attn_norm_g[0], tn=D_MODEL, name="attn_q_proj")
    sinks = jnp.broadcast_to(attn_sinks[0].astype(F32)[:, None], (N_HEADS, LANES))
    nblk = SEQ // WINDOW
    i = jnp.arange(WINDOW)[:, None]
    j = jnp.arange(2 * WINDOW)[None, :]
    dist = WINDOW + i - j
    in_window = (dist >= 0) & (dist < WINDOW)
    bias_p = _rel_bias(dist, rel_bias)
    bias_prompt = jnp.stack([jnp.where(in_window & (j >= WINDOW), bias_p, NEG_BIG),
                             jnp.where(in_window, bias_p, NEG_BIG)])
    i_s = jnp.arange(DEC_SEQ)[:, None]
    dist_s = WINDOW + i_s - j
    bias_sample = jnp.where((dist_s >= 0) & (dist_s < WINDOW), _rel_bias(dist_s, rel_bias), NEG_BIG)[None]

    a_p = _attention(
        q_all, k_all, k_all, v_all, v_all, bias_prompt, sinks, q_norm_g[0],
        grid=(BATCH, nblk), lq=WINDOW, lc=WINDOW,
        q_map=lambda b, n: (b * nblk + n, 0),
        prev_map=lambda b, n: (b * nblk + jnp.maximum(n - 1, 0), 0),
        cur_map=lambda b, n: (b * nblk + n, 0),
        bias_map=lambda b, n: (jnp.minimum(n, 1), 0, 0, 0),
        out_rows=T_PROMPT, out_map=lambda b, n: (b * nblk + n, 0),
        name="attn_prompt")
    cache_k = cache_k_window.reshape(DEC_BATCH * WINDOW, KV_DIM).astype(F32)
    cache_v = cache_v_window.reshape(DEC_BATCH * WINDOW, KV_DIM).astype(F32)
    a_s = _attention(
        q_all, cache_k, k_all, cache_v, v_all, bias_sample, sinks, q_norm_g[0],
        grid=(DEC_BATCH,), lq=DEC_SEQ, lc=DEC_SEQ,
        q_map=lambda b: (T_PROMPT // DEC_SEQ + b, 0),
        prev_map=lambda b: (b, 0),
        cur_map=lambda b: (T_PROMPT // DEC_SEQ + b, 0),
        bias_map=lambda b: (0, 0, 0, 0),
        out_rows=T_SAMPLE, out_map=lambda b: (b, 0),
        name="attn_sample")
    h = _matmul([a_p, a_s], w_o[0].astype(BF16), res=[h], tn=D_MODEL, name="attn_o_proj")

    h_p, h_s = _moe(h, moe_norm_g[1], moe_w_router[1], moe_b_router[1], 1, moe_w_gate, moe_b_gate,
                    moe_w_up, moe_b_up, moe_w_down, moe_b_down, split_rows=T_PROMPT)

    y_prompt = h_p.reshape(BATCH, SEQ, D_MODEL)
    y_sample = h_s.reshape(DEC_BATCH, DEC_SEQ, D_MODEL)
    k_p = k_all[:T_PROMPT].reshape(BATCH, SEQ, KV_DIM)[:, -WINDOW:].reshape(BATCH, WINDOW, N_KV_HEADS, HEAD_DIM)
    v_p = v_all[:T_PROMPT].reshape(BATCH, SEQ, KV_DIM)[:, -WINDOW:].reshape(BATCH, WINDOW, N_KV_HEADS, HEAD_DIM)
    k_new = k_all[T_PROMPT:].reshape(DEC_BATCH, DEC_SEQ, N_KV_HEADS, HEAD_DIM)
    v_new = v_all[T_PROMPT:].reshape(DEC_BATCH, DEC_SEQ, N_KV_HEADS, HEAD_DIM)
    k_s = jnp.concatenate([cache_k_window.astype(F32), k_new], axis=1)[:, -WINDOW:]
    v_s = jnp.concatenate([cache_v_window.astype(F32), v_new], axis=1)[:, -WINDOW:]
    ssm_shape = (1, -1, N_SSM_HEADS, SSM_HEAD_DIM, D_STATE)
    return (y_prompt, y_sample, ssm_p.reshape(ssm_shape), conv_p[None], k_p, v_p,
            ssm_s.reshape(ssm_shape), conv_s[None], k_s, v_s)
```
